```python
import jax, jax.numpy as jnp
from jax import lax
import numpy as np

D_MODEL = 1024
BATCH = 32
SEQ = 2048
DEPTH = 1

HEAD_DIM = 64
ATT_Q_HEADS = 8
ATT_KV_HEADS = 2
ATT_GROUP = ATT_Q_HEADS // ATT_KV_HEADS
WINDOW = 128
ROPE_THETA = 10000.0
SGU_HEADS = 8
SGU_CHUNK = 128
ATT_WIDTH = ATT_Q_HEADS * HEAD_DIM
KV_WIDTH = ATT_KV_HEADS * HEAD_DIM
SGU_WIDTH = SGU_HEADS * HEAD_DIM
MIX_WIDTH = ATT_WIDTH + SGU_WIDTH
IN_COLS = ATT_WIDTH + 2 * KV_WIDTH + 2 * SGU_WIDTH
SPLITS = (ATT_WIDTH, ATT_WIDTH + KV_WIDTH, ATT_WIDTH + 2 * KV_WIDTH,
          ATT_WIDTH + 2 * KV_WIDTH + SGU_WIDTH)
PEER_HEADS = 8
PEER_N_KEYS = 128
PEER_N_EXPERTS = PEER_N_KEYS * PEER_N_KEYS
PEER_TOPK = 16
PEER_QDIM = 256
PEER_TOKEN_BLOCK = 128
EPS = 1e-6

kernel_name = "hybrid_sgu_swa_sink_peer_block"


def rms_norm(x, g):
    xf = x.astype(jnp.float32)
    y = xf * lax.rsqrt(jnp.mean(xf * xf, axis=-1, keepdims=True) + EPS)
    return (y * g.astype(jnp.float32)).astype(x.dtype)


def layer_norm(x, g, b):
    xf = x.astype(jnp.float32)
    mu = jnp.mean(xf, axis=-1, keepdims=True)
    var = jnp.mean(jnp.square(xf - mu), axis=-1, keepdims=True)
    y = (xf - mu) * lax.rsqrt(var + EPS) * g.astype(jnp.float32) + b.astype(jnp.float32)
    return y.astype(x.dtype)


def rope(x, pos):
    half = HEAD_DIM // 2
    inv = ROPE_THETA ** (-jnp.arange(half, dtype=jnp.float32) / half)
    ang = pos.astype(jnp.float32)[:, None] * inv[None, :]
    cos = jnp.cos(ang)[None, :, None, :]
    sin = jnp.sin(ang)[None, :, None, :]
    xf = x.astype(jnp.float32)
    x1, x2 = xf[..., :half], xf[..., half:]
    out = jnp.concatenate([x1 * cos - x2 * sin, x2 * cos + x1 * sin], axis=-1)
    return out.astype(x.dtype)


def sliding_window_attention(q, k, v, sinks):
    B, S = q.shape[0], q.shape[1]
    nb = S // WINDOW
    qb = q.reshape(B, nb, WINDOW, ATT_KV_HEADS, ATT_GROUP, HEAD_DIM)
    kb = k.reshape(B, nb, WINDOW, ATT_KV_HEADS, HEAD_DIM)
    vb = v.reshape(B, nb, WINDOW, ATT_KV_HEADS, HEAD_DIM)
    pad = ((0, 0), (1, 0), (0, 0), (0, 0), (0, 0))
    kcat = jnp.concatenate([jnp.pad(kb[:, :-1], pad), kb], axis=2)
    vcat = jnp.concatenate([jnp.pad(vb[:, :-1], pad), vb], axis=2)
    s = jnp.einsum('bnqhgd,bnkhd->bnhgqk', qb, kcat,
                   preferred_element_type=jnp.float32) * (HEAD_DIM ** -0.5)
    blk = jnp.arange(nb)[:, None, None] * WINDOW
    qpos = blk + jnp.arange(WINDOW)[None, :, None]
    kpos = blk - WINDOW + jnp.arange(2 * WINDOW)[None, None, :]
    diff = qpos - kpos
    valid = (diff >= 0) & (diff < WINDOW) & (kpos >= 0)
    s = jnp.where(valid[None, :, None, None, :, :], s, -jnp.inf)
    sink = sinks.astype(jnp.float32).reshape(ATT_KV_HEADS, ATT_GROUP)[None, None, :, :, None, None]
    m = jnp.maximum(jnp.max(s, axis=-1, keepdims=True), sink)
    p = jnp.exp(s - m)
    p = p / (jnp.sum(p, axis=-1, keepdims=True) + jnp.exp(sink - m))
    o = jnp.einsum('bnhgqk,bnkhd->bnqhgd', p.astype(v.dtype), vcat)
    return o.reshape(B, S, ATT_WIDTH).astype(q.dtype)


def chunked_spatial_gating(u, v, w_s, b_s, ln_g, ln_b):
    B, S = u.shape[0], u.shape[1]
    nc = S // SGU_CHUNK
    vn = layer_norm(v, ln_g, ln_b)
    vc = vn.reshape(B, nc, SGU_CHUNK, SGU_HEADS, HEAD_DIM)
    causal = jnp.tril(jnp.ones((SGU_CHUNK, SGU_CHUNK), dtype=bool))
    w = jnp.where(causal[None], w_s, 0).astype(v.dtype)
    mixed = jnp.einsum('hts,bcshd->bcthd', w, vc) + b_s.T.astype(v.dtype)[None, None, :, :, None]
    return (u * mixed.reshape(u.shape)).reshape(B, S, SGU_WIDTH)


def hybrid_mixer(h, w_in, w_s, b_s, sgu_ln_g, sgu_ln_b, sinks, g_att_out, g_sgu_out, w_out):
    B, S, _ = h.shape
    proj = h @ w_in
    q, k, v, u_s, v_s = jnp.split(proj, SPLITS, axis=-1)
    pos = jnp.arange(S)
    q = rope(q.reshape(B, S, ATT_Q_HEADS, HEAD_DIM), pos)
    k = rope(k.reshape(B, S, ATT_KV_HEADS, HEAD_DIM), pos)
    v = v.reshape(B, S, ATT_KV_HEADS, HEAD_DIM)
    att = sliding_window_attention(q, k, v, sinks)
    u_s = jax.nn.gelu(u_s, approximate=False).reshape(B, S, SGU_HEADS, HEAD_DIM)
    v_s = jax.nn.gelu(v_s, approximate=False).reshape(B, S, SGU_HEADS, HEAD_DIM)
    sgu = chunked_spatial_gating(u_s, v_s, w_s, b_s, sgu_ln_g, sgu_ln_b)
    y = jnp.concatenate([rms_norm(att, g_att_out), rms_norm(sgu, g_sgu_out)], axis=-1)
    return y @ w_out


def peer_ffn(h, w_query, sub_keys, expert_down, expert_up):
    B, S, D = h.shape
    T = B * S
    ht = h.reshape(T, D)
    q = (ht @ w_query).reshape(T, PEER_HEADS, 2, PEER_QDIM // 2)
    sc = jnp.einsum('thpd,hpnd->thpn', q, sub_keys, preferred_element_type=jnp.float32)
    sub_s, sub_i = lax.top_k(sc, PEER_TOPK)
    cand_s = sub_s[:, :, 0, :, None] + sub_s[:, :, 1, None, :]
    cand_i = sub_i[:, :, 0, :, None] * PEER_N_KEYS + sub_i[:, :, 1, None, :]
    top_s, top_j = lax.top_k(cand_s.reshape(T, PEER_HEADS, PEER_TOPK * PEER_TOPK), PEER_TOPK)
    experts = jnp.take_along_axis(cand_i.reshape(T, PEER_HEADS, PEER_TOPK * PEER_TOPK), top_j, axis=-1)
    gates = jax.nn.softmax(top_s, axis=-1)
    nblk = T // PEER_TOKEN_BLOCK
    HK = PEER_HEADS * PEER_TOPK

    def block(args):
        xb, eb, gb = args
        u = jnp.take(expert_down, eb, axis=0)
        a = jax.nn.gelu(jnp.einsum('tkd,td->tk', u, xb), approximate=False)
        w = jnp.take(expert_up, eb, axis=0)
        return jnp.einsum('tk,tkd->td', gb * a, w)

    out = lax.map(block, (ht.reshape(nblk, PEER_TOKEN_BLOCK, D),
                          experts.reshape(nblk, PEER_TOKEN_BLOCK, HK),
                          gates.reshape(nblk, PEER_TOKEN_BLOCK, HK).astype(h.dtype)))
    return out.reshape(B, S, D)


def setup_inputs(seed: int = 0) -> dict:
    key = jax.random.key(seed)
    ks = jax.random.split(key, 24)
    f32 = jnp.float32
    L, D = DEPTH, D_MODEL
    nrm = lambda k, shape, s: jax.random.normal(k, shape, f32) * s
    gain = lambda k, shape: 1.0 + 0.05 * jax.random.normal(k, shape, f32)
    return {
        "x": jax.random.normal(ks[0], (BATCH, SEQ, D), f32),
        "c": jax.random.normal(ks[1], (BATCH, D), f32),
        "w_ada": nrm(ks[2], (L, D, 6 * D), 0.5 * D ** -0.5),
        "b_ada": nrm(ks[3], (L, 6 * D), 0.02),
        "g_pre_mix": gain(ks[4], (L, D)),
        "g_post_mix": gain(ks[5], (L, D)),
        "g_pre_ffn": gain(ks[6], (L, D)),
        "g_post_ffn": gain(ks[7], (L, D)),
        "w_in": nrm(ks[8], (L, D, IN_COLS), D ** -0.5),
        "w_s": nrm(ks[9], (L, SGU_HEADS, SGU_CHUNK, SGU_CHUNK), SGU_CHUNK ** -0.5),
        "b_s": 1.0 + nrm(ks[10], (L, SGU_HEADS, SGU_CHUNK), 0.02),
        "sgu_ln_g": gain(ks[11], (L, SGU_HEADS, HEAD_DIM)),
        "sgu_ln_b": nrm(ks[12], (L, SGU_HEADS, HEAD_DIM), 0.02),
        "sinks": nrm(ks[13], (L, ATT_Q_HEADS), 0.5),
        "g_att_out": gain(ks[14], (L, ATT_WIDTH)),
        "g_sgu_out": gain(ks[15], (L, SGU_WIDTH)),
        "w_out": nrm(ks[16], (L, MIX_WIDTH, D), MIX_WIDTH ** -0.5),
        "w_query": nrm(ks[17], (L, D, PEER_HEADS * PEER_QDIM), D ** -0.5),
        "sub_keys": nrm(ks[18], (L, PEER_HEADS, 2, PEER_N_KEYS, PEER_QDIM // 2), (PEER_QDIM // 2) ** -0.5),
        "expert_down": nrm(ks[19], (L, PEER_N_EXPERTS, D), D ** -0.5),
        "expert_up": nrm(ks[20], (L, PEER_N_EXPERTS, D), PEER_HEADS ** -0.5),
    }


def reference(x, c, w_ada, b_ada, g_pre_mix, g_post_mix, g_pre_ffn, g_post_ffn,
              w_in, w_s, b_s, sgu_ln_g, sgu_ln_b, sinks, g_att_out, g_sgu_out, w_out,
              w_query, sub_keys, expert_down, expert_up):
    c_act = jax.nn.silu(c)
    for l in range(DEPTH):
        mod = c_act @ w_ada[l] + b_ada[l]
        sh1, sc1, gt1, sh2, sc2, gt2 = [m[:, None, :] for m in jnp.split(mod, 6, axis=-1)]
        h = rms_norm(x, g_pre_mix[l]) * (1.0 + sc1) + sh1
        y = hybrid_mixer(h, w_in[l], w_s[l], b_s[l], sgu_ln_g[l], sgu_ln_b[l], sinks[l],
                         g_att_out[l], g_sgu_out[l], w_out[l])
        x = x + gt1 * rms_norm(y, g_post_mix[l])
        h = rms_norm(x, g_pre_ffn[l]) * (1.0 + sc2) + sh2
        y = peer_ffn(h, w_query[l], sub_keys[l], expert_down[l], expert_up[l])
        x = x + gt2 * rms_norm(y, g_post_ffn[l])
    return x
```

```python
import functools

import jax
import jax.numpy as jnp
from jax import lax
from jax.experimental import pallas as pl
from jax.experimental.pallas import tpu as pltpu

HEAD_DIM = 64
ATT_Q_HEADS = 8
ATT_KV_HEADS = 2
WINDOW = 128
ROPE_THETA = 10000.0
SGU_HEADS = 8
SGU_CHUNK = 128
ATT_WIDTH = ATT_Q_HEADS * HEAD_DIM
KV_WIDTH = ATT_KV_HEADS * HEAD_DIM
SGU_WIDTH = SGU_HEADS * HEAD_DIM
IN_COLS = ATT_WIDTH + 2 * KV_WIDTH + 2 * SGU_WIDTH
PEER_HEADS = 8
PEER_N_KEYS = 128
PEER_TOPK = 16
PEER_QDIM = 256
EPS = 1e-6

LANES = 128
VMEM_LIMIT_BYTES = 56 * 1024 * 1024

F32 = jnp.float32
BF16 = jnp.bfloat16
NEG_INF = float("-inf")


def _rms(x):
    return x * lax.rsqrt(jnp.mean(x * x, axis=-1, keepdims=True) + EPS)


def _gelu(t):
    return 0.5 * t * (1.0 + lax.erf(t * 0.7071067811865476))


def _split_bf16(t):
    hi = t.astype(BF16)
    lo = (t - hi.astype(F32)).astype(BF16)
    return hi, lo


def _ada_kernel(c_ref, w_ref, b_ref, o_ref):
    c = c_ref[...]
    c_act = c * (1.0 / (1.0 + jnp.exp(-c)))
    c_hi, c_lo = _split_bf16(c_act)
    w_hi, w_lo = _split_bf16(w_ref[...])
    acc = jnp.dot(c_hi, w_hi, preferred_element_type=F32)
    acc += jnp.dot(c_lo, w_hi, preferred_element_type=F32)
    acc += jnp.dot(c_hi, w_lo, preferred_element_type=F32)
    o_ref[...] = acc + b_ref[...]


def _ada_modulation(c, w_ada, b_ada):
    bsz, d = c.shape
    n = w_ada.shape[1]
    bn = 1024
    return pl.pallas_call(
        _ada_kernel,
        out_shape=jax.ShapeDtypeStruct((bsz, n), F32),
        grid=(n // bn,),
        in_specs=[
            pl.BlockSpec((bsz, d), lambda j: (0, 0)),
            pl.BlockSpec((d, bn), lambda j: (0, j)),
            pl.BlockSpec((1, bn), lambda j: (0, j)),
        ],
        out_specs=pl.BlockSpec((bsz, bn), lambda j: (0, j)),
        compiler_params=pltpu.CompilerParams(
            dimension_semantics=("arbitrary",), vmem_limit_bytes=VMEM_LIMIT_BYTES),
        name="ada_modulation",
    )(c, w_ada, b_ada.reshape(1, n))


def _mixer_kernel(sinks_ref, x_ref, mod_ref, gpre_ref, gpost_ref, win_ref, wout_ref,
                  cos_ref, sina_ref, sinb_ref, ws_ref, bias_ref, lng_ref, lnb_ref,
                  gatt_ref, gsgu_ref, o_ref,
                  q_scr, kd_scr, vd_scr, u_scr, vn_scr, att_scr, sgu_scr, *, ts):
    s_idx = pl.program_id(1)
    n_chunks = ts // WINDOW

    x = x_ref[...]
    sh1 = mod_ref[0:1, :]
    sc1 = mod_ref[1:2, :]
    gt1 = mod_ref[2:3, :]
    h = _rms(x) * gpre_ref[...] * (1.0 + sc1) + sh1
    proj = jnp.dot(h.astype(BF16), win_ref[...], preferred_element_type=F32)

    lane = lax.broadcasted_iota(jnp.int32, (ts, LANES), 1)
    lo_half = lane < HEAD_DIM

    cos = cos_ref[...]
    sina = sina_ref[...]
    sinb = sinb_ref[...]

    def rope(t):
        return (t * cos + pltpu.roll(t, LANES - HEAD_DIM // 2, 1) * sina
                + pltpu.roll(t, HEAD_DIM // 2, 1) * sinb)

    for c in range(ATT_WIDTH // LANES):
        qc = rope(proj[:, c * LANES:(c + 1) * LANES]) * (HEAD_DIM ** -0.5)
        q_scr[0, :, c * LANES:(c + 1) * LANES] = jnp.where(lo_half, qc, 0.0).astype(BF16)
        q_scr[1, :, c * LANES:(c + 1) * LANES] = jnp.where(lo_half, 0.0, qc).astype(BF16)

    k = rope(proj[:, ATT_WIDTH:ATT_WIDTH + KV_WIDTH])
    v = proj[:, ATT_WIDTH + KV_WIDTH:ATT_WIDTH + 2 * KV_WIDTH]
    k_sw = pltpu.roll(k, HEAD_DIM, 1)
    v_sw = pltpu.roll(v, HEAD_DIM, 1)

    @pl.when(s_idx == 0)
    def _():
        kd_scr[:, 0:WINDOW, :] = jnp.zeros((2, WINDOW, LANES), BF16)
        vd_scr[:, 0:WINDOW, :] = jnp.zeros((2, WINDOW, LANES), BF16)

    kd_scr[0, WINDOW:WINDOW + ts, :] = jnp.where(lo_half, k, k_sw).astype(BF16)
    kd_scr[1, WINDOW:WINDOW + ts, :] = jnp.where(lo_half, k_sw, k).astype(BF16)
    vd_scr[0, WINDOW:WINDOW + ts, :] = jnp.where(lo_half, v, v_sw).astype(BF16)
    vd_scr[1, WINDOW:WINDOW + ts, :] = jnp.where(lo_half, v_sw, v).astype(BF16)

    u_scr[...] = _gelu(proj[:, ATT_WIDTH + 2 * KV_WIDTH:ATT_WIDTH + 2 * KV_WIDTH + SGU_WIDTH])
    vs = _gelu(proj[:, ATT_WIDTH + 2 * KV_WIDTH + SGU_WIDTH:IN_COLS])
    ri = lax.broadcasted_iota(jnp.int32, (LANES, LANES), 0)
    ci = lax.broadcasted_iota(jnp.int32, (LANES, LANES), 1)
    avg = jnp.where((ri // HEAD_DIM) == (ci // HEAD_DIM), 1.0 / HEAD_DIM, 0.0).astype(BF16)

    def group_mean(t):
        hi, lo = _split_bf16(t)
        return (jnp.dot(hi, avg, preferred_element_type=F32)
                + jnp.dot(lo, avg, preferred_element_type=F32))

    for c in range(SGU_WIDTH // LANES):
        sl = slice(c * LANES, (c + 1) * LANES)
        vc = vs[:, sl]
        d = vc - group_mean(vc)
        vn = d * lax.rsqrt(group_mean(d * d) + EPS) * lng_ref[:, sl] + lnb_ref[:, sl]
        vn_scr[0, :, sl] = jnp.where(lo_half, vn, 0.0).astype(BF16)
        vn_scr[1, :, sl] = jnp.where(lo_half, 0.0, vn).astype(BF16)

    tril = ci <= ri
    w_mix = [jnp.where(tril, ws_ref[hh], 0.0).astype(BF16) for hh in range(SGU_HEADS)]

    qi = lax.broadcasted_iota(jnp.int32, (WINDOW, 2 * WINDOW), 0)
    kj = lax.broadcasted_iota(jnp.int32, (WINDOW, 2 * WINDOW), 1)
    band = (kj > qi) & (kj <= qi + WINDOW)
    lo_half_w = lax.broadcasted_iota(jnp.int32, (WINDOW, LANES), 1) < HEAD_DIM

    def chunk_body(n, carry):
        r0 = pl.multiple_of(n * WINDOW, WINDOW)
        first = jnp.logical_and(s_idx == 0, n == 0)
        min_key = jnp.where(first, WINDOW, 0)
        mask = band & (kj >= min_key)
        for col in range(ATT_WIDTH // LANES):
            g = col // (ATT_Q_HEADS // ATT_KV_HEADS // 2)
            kd = kd_scr[g, pl.ds(r0, 2 * WINDOW), :]
            vd = vd_scr[g, pl.ds(r0, 2 * WINDOW), :]
            outs = []
            for side in range(2):
                hh = 2 * col + side
                qm = q_scr[side, pl.ds(r0, WINDOW), col * LANES:(col + 1) * LANES]
                s = lax.dot_general(qm, kd, (((1,), (1,)), ((), ())),
                                    preferred_element_type=F32)
                s = jnp.where(mask, s, NEG_INF)
                sink = sinks_ref[hh]
                m = jnp.maximum(jnp.max(s, axis=1, keepdims=True), sink)
                p = jnp.exp(s - m)
                den = jnp.sum(p, axis=1, keepdims=True) + jnp.exp(sink - m)
                o = jnp.dot(p.astype(BF16), vd, preferred_element_type=F32)
                outs.append(o / den)
            att_scr[pl.ds(r0, WINDOW), col * LANES:(col + 1) * LANES] = jnp.where(
                lo_half_w, outs[0], outs[1])
        for c in range(SGU_WIDTH // LANES):
            sl = slice(c * LANES, (c + 1) * LANES)
            mixed = (jnp.dot(w_mix[2 * c], vn_scr[0, pl.ds(r0, SGU_CHUNK), sl],
                             preferred_element_type=F32)
                     + jnp.dot(w_mix[2 * c + 1], vn_scr[1, pl.ds(r0, SGU_CHUNK), sl],
                               preferred_element_type=F32))
            sgu_scr[pl.ds(r0, SGU_CHUNK), sl] = (
                u_scr[pl.ds(r0, SGU_CHUNK), sl] * (mixed + bias_ref[:, sl]))
        return carry

    lax.fori_loop(0, n_chunks, chunk_body, 0)

    kd_scr[:, 0:WINDOW, :] = kd_scr[:, ts:ts + WINDOW, :]
    vd_scr[:, 0:WINDOW, :] = vd_scr[:, ts:ts + WINDOW, :]

    ya = (_rms(att_scr[...]) * gatt_ref[...]).astype(BF16)
    ys = (_rms(sgu_scr[...]) * gsgu_ref[...]).astype(BF16)
    y = (jnp.dot(ya, wout_ref[0:ATT_WIDTH, :], preferred_element_type=F32)
         + jnp.dot(ys, wout_ref[ATT_WIDTH:ATT_WIDTH + SGU_WIDTH, :], preferred_element_type=F32))
    o_ref[...] = x + gt1 * (_rms(y) * gpost_ref[...])


def _rope_tables(seq):
    half = HEAD_DIM // 2
    inv = ROPE_THETA ** (-jnp.arange(half, dtype=F32) / half)
    ang = jnp.arange(seq, dtype=F32)[:, None] * inv[None, :]
    d = jnp.arange(LANES) % HEAD_DIM
    cos = jnp.cos(ang)[:, d % half]
    sin = jnp.sin(ang)[:, d % half]
    first = (d < half)[None, :]
    return cos, jnp.where(first, -sin, 0.0), jnp.where(first, 0.0, sin)


def _mixer(x, mod3, g_pre, g_post, w_in, w_s, b_s, ln_g, ln_b, sinks, g_att, g_sgu, w_out, ts):
    bsz, seq, d = x.shape
    cos, sina, sinb = _rope_tables(seq)
    bias_full = jnp.repeat(b_s.T, HEAD_DIM, axis=1)
    row = lambda a: a.reshape(1, -1)
    full = lambda shape: pl.BlockSpec(shape, lambda b, s: (0,) * len(shape))
    return pl.pallas_call(
        functools.partial(_mixer_kernel, ts=ts),
        out_shape=jax.ShapeDtypeStruct((bsz, seq, d), F32),
        grid=(bsz, seq // ts),
        in_specs=[
            pl.BlockSpec(memory_space=pltpu.SMEM),
            pl.BlockSpec((None, ts, d), lambda b, s: (b, s, 0)),
            pl.BlockSpec((None, 6, d), lambda b, s: (b, 0, 0)),
            full((1, d)), full((1, d)),
            full((d, IN_COLS)), full((ATT_WIDTH + SGU_WIDTH, d)),
            pl.BlockSpec((ts, LANES), lambda b, s: (s, 0)),
            pl.BlockSpec((ts, LANES), lambda b, s: (s, 0)),
            pl.BlockSpec((ts, LANES), lambda b, s: (s, 0)),
            full((SGU_HEADS, SGU_CHUNK, SGU_CHUNK)),
            full((SGU_CHUNK, SGU_WIDTH)),
            full((1, SGU_WIDTH)), full((1, SGU_WIDTH)),
            full((1, ATT_WIDTH)), full((1, SGU_WIDTH)),
        ],
        out_specs=pl.BlockSpec((None, ts, d), lambda b, s: (b, s, 0)),
        scratch_shapes=[
            pltpu.VMEM((2, ts, ATT_WIDTH), BF16),
            pltpu.VMEM((2, ts + WINDOW, LANES), BF16),
            pltpu.VMEM((2, ts + WINDOW, LANES), BF16),
            pltpu.VMEM((ts, SGU_WIDTH), F32),
            pltpu.VMEM((2, ts, SGU_WIDTH), BF16),
            pltpu.VMEM((ts, ATT_WIDTH), F32),
            pltpu.VMEM((ts, SGU_WIDTH), F32),
        ],
        compiler_params=pltpu.CompilerParams(
            dimension_semantics=("arbitrary", "arbitrary"), vmem_limit_bytes=VMEM_LIMIT_BYTES),
        name="token_mixer",
    )(sinks, x, mod3, row(g_pre), row(g_post), w_in.astype(BF16), w_out.astype(BF16),
      cos, sina, sinb, w_s, bias_full, row(ln_g), row(ln_b), row(g_att), row(g_sgu))


def _top_rows(a, k, with_rank=False):
    rows = []
    x = a
    rank = jnp.full(a.shape, float(k), F32) if with_rank else None
    for r in range(k):
        m = jnp.max(x, axis=0, keepdims=True)
        rows.append(m)
        if with_rank or r + 1 < k:
            hit = x == m
            if with_rank:
                rank = jnp.where(hit, float(r), rank)
            x = jnp.where(hit, NEG_INF, x)
    return (rows, rank) if with_rank else rows


def _pack_rows(rows):
    n = len(rows)
    rid = lax.broadcasted_iota(jnp.int32, (n, LANES), 0)
    out = jnp.broadcast_to(rows[0], (n, LANES))
    for r in range(1, n):
        out = jnp.where(rid == r, rows[r], out)
    return out


def _route_lane_group(a1, a2):
    k = PEER_TOPK
    t1 = _top_rows(a1, k)
    t2, rank2 = _top_rows(a2, k, with_rank=True)
    t2p = _pack_rows(t2)
    t1_hi = _pack_rows(t1[k // 2:])
    rid8 = lax.broadcasted_iota(jnp.int32, (8, LANES), 0)
    pieces = [t1[0] + t2p]
    for a in range(1, k // 2):
        nb = k // (a + 1)
        cnd = t1[a] + t2p[0:8]
        if nb < 8:
            cnd = jnp.where(rid8 < nb, cnd, NEG_INF)
        pieces.append(cnd)
    pieces.append(t1_hi + t2[0])
    cand = jnp.concatenate(pieces, axis=0)
    top = _top_rows(cand, k)
    thr = top[k - 1]
    z = jnp.sum(jnp.where(cand >= thr, jnp.exp(cand - top[0]), 0.0), axis=0, keepdims=True)
    count1 = jnp.zeros(a1.shape, F32)
    for b in range(k):
        count1 = count1 + jnp.where(a1 + t2[b] >= thr, 1.0, 0.0)
    e1 = jnp.exp(a1 - t1[0]) / z
    e2 = jnp.exp(a2 - t2[0])
    return count1, e1, rank2, e2


def _peer_kernel(x_ref, mod_ref, gpre_ref, gpost_ref, wqt_ref, sk_ref, ed_ref, eut_ref, o_ref,
                 xt_scr, acc_scr, s_scr, n1_scr, e1_scr, r2_scr, e2_scr, at_scr, h_scr,
                 *, tb, ec):
    j = pl.program_id(1)
    n_lg = tb // LANES
    n_i = ec // PEER_N_KEYS

    @pl.when(j == 0)
    def _prologue():
        x = x_ref[...]
        sh2 = mod_ref[3:4, :]
        sc2 = mod_ref[4:5, :]
        h = _rms(x) * gpre_ref[...] * (1.0 + sc2) + sh2
        xt_scr[...] = h.T.astype(BF16)
        acc_scr[...] = jnp.zeros_like(acc_scr)

        def head_body(hh, carry):
            r0 = pl.multiple_of(hh * PEER_QDIM, PEER_QDIM)
            qt = jnp.dot(wqt_ref[pl.ds(r0, PEER_QDIM), :], xt_scr[...],
                         preferred_element_type=F32)
            half = PEER_QDIM // 2
            s1 = jnp.dot(sk_ref[2 * hh], qt[0:half].astype(BF16), preferred_element_type=F32)
            s2 = jnp.dot(sk_ref[2 * hh + 1], qt[half:].astype(BF16), preferred_element_type=F32)
            for lg in range(n_lg):
                s_scr[0, lg] = s1[:, lg * LANES:(lg + 1) * LANES]
                s_scr[1, lg] = s2[:, lg * LANES:(lg + 1) * LANES]

            def lg_body(lg, c2):
                count1, e1, rank2, e2 = _route_lane_group(s_scr[0, lg], s_scr[1, lg])
                n1_scr[hh, lg] = count1
                e1_scr[hh, lg] = e1
                r2_scr[hh, lg] = rank2
                e2_scr[hh, lg] = e2
                return c2

            lax.fori_loop(0, n_lg, lg_body, 0)
            return carry

        lax.fori_loop(0, PEER_HEADS, head_body, 0)

    at_scr[...] = jnp.dot(ed_ref[...], xt_scr[...], preferred_element_type=F32)

    def i_body(ii, carry):
        i = j * n_i + ii
        r0 = pl.multiple_of(ii * PEER_N_KEYS, PEER_N_KEYS)
        for lg in range(n_lg):
            sl = slice(lg * LANES, (lg + 1) * LANES)
            gate = jnp.zeros((PEER_N_KEYS, LANES), F32)
            for hh in range(PEER_HEADS):
                n1 = n1_scr[hh, lg, pl.ds(i, 1), :]
                e1 = e1_scr[hh, lg, pl.ds(i, 1), :]
                gate = gate + jnp.where(r2_scr[hh, lg] < n1, e2_scr[hh, lg], 0.0) * e1
            a = at_scr[pl.ds(r0, PEER_N_KEYS), sl]
            h_scr[pl.ds(r0, PEER_N_KEYS), sl] = (_gelu(a) * gate).astype(BF16)
        return carry

    lax.fori_loop(0, n_i, i_body, 0)

    acc_scr[...] += jnp.dot(eut_ref[...], h_scr[...], preferred_element_type=F32)

    @pl.when(j == pl.num_programs(1) - 1)
    def _epilogue():
        y = acc_scr[...].T
        gt2 = mod_ref[5:6, :]
        o_ref[...] = x_ref[...] + gt2 * (_rms(y) * gpost_ref[...])


def _peer(x1, mod3, g_pre, g_post, w_query, sub_keys, expert_down, expert_up, seq, tb, ec):
    t, d = x1.shape
    ne = expert_down.shape[0]
    n_lg = tb // LANES
    wqt = w_query.T.astype(BF16)
    sk = sub_keys.reshape(PEER_HEADS * 2, PEER_N_KEYS, PEER_QDIM // 2).astype(BF16)
    ed = expert_down.astype(BF16)
    eut = expert_up.T.astype(BF16)
    row = lambda a: a.reshape(1, -1)
    full = lambda shape: pl.BlockSpec(shape, lambda i, j: (0,) * len(shape))
    tiles_per_seq = seq // tb
    route_scr = pltpu.VMEM((PEER_HEADS, n_lg, PEER_N_KEYS, LANES), F32)
    return pl.pallas_call(
        functools.partial(_peer_kernel, tb=tb, ec=ec),
        out_shape=jax.ShapeDtypeStruct((t, d), F32),
        grid=(t // tb, ne // ec),
        in_specs=[
            pl.BlockSpec((tb, d), lambda i, j: (i, 0)),
            pl.BlockSpec((None, 6, d), lambda i, j: (i // tiles_per_seq, 0, 0)),
            full((1, d)), full((1, d)),
            full((PEER_HEADS * PEER_QDIM, d)),
            full((PEER_HEADS * 2, PEER_N_KEYS, PEER_QDIM // 2)),
            pl.BlockSpec((ec, d), lambda i, j: (j, 0)),
            pl.BlockSpec((d, ec), lambda i, j: (0, j)),
        ],
        out_specs=pl.BlockSpec((tb, d), lambda i, j: (i, 0)),
        scratch_shapes=[
            pltpu.VMEM((d, tb), BF16),
            pltpu.VMEM((d, tb), F32),
            pltpu.VMEM((2, n_lg, PEER_N_KEYS, LANES), F32),
            route_scr, route_scr, route_scr, route_scr,
            pltpu.VMEM((ec, tb), F32),
            pltpu.VMEM((ec, tb), BF16),
        ],
        compiler_params=pltpu.CompilerParams(
            dimension_semantics=("arbitrary", "arbitrary"), vmem_limit_bytes=VMEM_LIMIT_BYTES),
        name="peer_dense",
    )(x1, mod3, row(g_pre), row(g_post), wqt, sk, ed, eut)


def _pick_tile(n, target):
    t = min(n, target)
    while n % t:
        t -= LANES
    return t


def kernel(x, c, w_ada, b_ada, g_pre_mix, g_post_mix, g_pre_ffn, g_post_ffn, w_in, w_s, b_s,
           sgu_ln_g, sgu_ln_b, sinks, g_att_out, g_sgu_out, w_out, w_query, sub_keys,
           expert_down, expert_up):
    bsz, seq, d = x.shape
    ts = _pick_tile(seq, 512)
    tb = _pick_tile(seq, 512)
    ec = _pick_tile(expert_down.shape[1], 512)
    for l in range(w_ada.shape[0]):
        mod3 = _ada_modulation(c, w_ada[l], b_ada[l]).reshape(bsz, 6, d)
        x = _mixer(x, mod3, g_pre_mix[l], g_post_mix[l], w_in[l], w_s[l], b_s[l],
                   sgu_ln_g[l].reshape(-1), sgu_ln_b[l].reshape(-1), sinks[l],
                   g_att_out[l], g_sgu_out[l], w_out[l], ts)
        y = _peer(x.reshape(bsz * seq, d), mod3, g_pre_ffn[l], g_post_ffn[l], w_query[l],
                  sub_keys[l], expert_down[l], expert_up[l], seq, tb, ec)
        x = y.reshape(bsz, seq, d)
    return x
```

```python
import functools

import jax
import jax.numpy as jnp
from jax import lax
from jax.experimental import pallas as pl
from jax.experimental.pallas import tpu as pltpu

HEAD_DIM = 64
ATT_Q_HEADS = 8
ATT_KV_HEADS = 2
WINDOW = 128
ROPE_THETA = 10000.0
SGU_HEADS = 8
SGU_CHUNK = 128
ATT_WIDTH = ATT_Q_HEADS * HEAD_DIM
KV_WIDTH = ATT_KV_HEADS * HEAD_DIM
SGU_WIDTH = SGU_HEADS * HEAD_DIM
IN_COLS = ATT_WIDTH + 2 * KV_WIDTH + 2 * SGU_WIDTH
PEER_HEADS = 8
PEER_N_KEYS = 128
PEER_TOPK = 16
PEER_QDIM = 256
EPS = 1e-6

LANES = 128
VMEM_LIMIT_BYTES = 56 * 1024 * 1024

F32 = jnp.float32
BF16 = jnp.bfloat16
NEG_INF = float("-inf")


def _rms(x):
    return x * lax.rsqrt(jnp.mean(x * x, axis=-1, keepdims=True) + EPS)


def _gelu(t):
    return 0.5 * t * (1.0 + lax.erf(t * 0.7071067811865476))


def _split_bf16(t):
    hi = t.astype(BF16)
    lo = (t - hi.astype(F32)).astype(BF16)
    return hi, lo


def _ada_kernel(c_ref, w_ref, b_ref, o_ref):
    c = c_ref[...]
    c_act = c * (1.0 / (1.0 + jnp.exp(-c)))
    c_hi, c_lo = _split_bf16(c_act)
    w_hi, w_lo = _split_bf16(w_ref[...])
    acc = jnp.dot(c_hi, w_hi, preferred_element_type=F32)
    acc += jnp.dot(c_lo, w_hi, preferred_element_type=F32)
    acc += jnp.dot(c_hi, w_lo, preferred_element_type=F32)
    o_ref[...] = acc + b_ref[...]


def _ada_modulation(c, w_ada, b_ada):
    bsz, d = c.shape
    n = w_ada.shape[1]
    bn = 1024
    return pl.pallas_call(
        _ada_kernel,
        out_shape=jax.ShapeDtypeStruct((bsz, n), F32),
        grid=(n // bn,),
        in_specs=[
            pl.BlockSpec((bsz, d), lambda j: (0, 0)),
            pl.BlockSpec((d, bn), lambda j: (0, j)),
            pl.BlockSpec((1, bn), lambda j: (0, j)),
        ],
        out_specs=pl.BlockSpec((bsz, bn), lambda j: (0, j)),
        compiler_params=pltpu.CompilerParams(
            dimension_semantics=("arbitrary",), vmem_limit_bytes=VMEM_LIMIT_BYTES),
        name="ada_modulation",
    )(c, w_ada, b_ada.reshape(1, n))


def _mixer_kernel(sinks_ref, x_ref, mod_ref, gpre_ref, gpost_ref, win_ref, wout_ref,
                  cos_ref, sina_ref, sinb_ref, ws_ref, bias_ref, lng_ref, lnb_ref,
                  gatt_ref, gsgu_ref, o_ref,
                  q_scr, kd_scr, vd_scr, u_scr, vn_scr, att_scr, sgu_scr, *, ts):
    s_idx = pl.program_id(1)
    n_chunks = ts // WINDOW

    x = x_ref[...]
    sh1 = mod_ref[0:1, :]
    sc1 = mod_ref[1:2, :]
    gt1 = mod_ref[2:3, :]
    h = _rms(x) * gpre_ref[...] * (1.0 + sc1) + sh1
    proj = jnp.dot(h.astype(BF16), win_ref[...], preferred_element_type=F32)

    lane = lax.broadcasted_iota(jnp.int32, (ts, LANES), 1)
    lo_half = lane < HEAD_DIM

    cos = cos_ref[...]
    sina = sina_ref[...]
    sinb = sinb_ref[...]

    def rope(t):
        return (t * cos + pltpu.roll(t, LANES - HEAD_DIM // 2, 1) * sina
                + pltpu.roll(t, HEAD_DIM // 2, 1) * sinb)

    for c in range(ATT_WIDTH // LANES):
        qc = rope(proj[:, c * LANES:(c + 1) * LANES]) * (HEAD_DIM ** -0.5)
        q_scr[0, :, c * LANES:(c + 1) * LANES] = jnp.where(lo_half, qc, 0.0).astype(BF16)
        q_scr[1, :, c * LANES:(c + 1) * LANES] = jnp.where(lo_half, 0.0, qc).astype(BF16)

    k = rope(proj[:, ATT_WIDTH:ATT_WIDTH + KV_WIDTH])
    v = proj[:, ATT_WIDTH + KV_WIDTH:ATT_WIDTH + 2 * KV_WIDTH]
    k_sw = pltpu.roll(k, HEAD_DIM, 1)
    v_sw = pltpu.roll(v, HEAD_DIM, 1)

    @pl.when(s_idx == 0)
    def _():
        kd_scr[:, 0:WINDOW, :] = jnp.zeros((2, WINDOW, LANES), BF16)
        vd_scr[:, 0:WINDOW, :] = jnp.zeros((2, WINDOW, LANES), BF16)

    kd_scr[0, WINDOW:WINDOW + ts, :] = jnp.where(lo_half, k, k_sw).astype(BF16)
    kd_scr[1, WINDOW:WINDOW + ts, :] = jnp.where(lo_half, k_sw, k).astype(BF16)
    vd_scr[0, WINDOW:WINDOW + ts, :] = jnp.where(lo_half, v, v_sw).astype(BF16)
    vd_scr[1, WINDOW:WINDOW + ts, :] = jnp.where(lo_half, v_sw, v).astype(BF16)

    u_scr[...] = _gelu(proj[:, ATT_WIDTH + 2 * KV_WIDTH:ATT_WIDTH + 2 * KV_WIDTH + SGU_WIDTH])
    vs = _gelu(proj[:, ATT_WIDTH + 2 * KV_WIDTH + SGU_WIDTH:IN_COLS])
    ri = lax.broadcasted_iota(jnp.int32, (LANES, LANES), 0)
    ci = lax.broadcasted_iota(jnp.int32, (LANES, LANES), 1)
    avg = jnp.where((ri // HEAD_DIM) == (ci // HEAD_DIM), 1.0 / HEAD_DIM, 0.0).astype(BF16)

    def group_mean(t):
        hi, lo = _split_bf16(t)
        return (jnp.dot(hi, avg, preferred_element_type=F32)
                + jnp.dot(lo, avg, preferred_element_type=F32))

    for c in range(SGU_WIDTH // LANES):
        sl = slice(c * LANES, (c + 1) * LANES)
        vc = vs[:, sl]
        d = vc - group_mean(vc)
        vn = d * lax.rsqrt(group_mean(d * d) + EPS) * lng_ref[:, sl] + lnb_ref[:, sl]
        vn_scr[0, :, sl] = jnp.where(lo_half, vn, 0.0).astype(BF16)
        vn_scr[1, :, sl] = jnp.where(lo_half, 0.0, vn).astype(BF16)

    tril = ci <= ri
    w_mix = [jnp.where(tril, ws_ref[hh], 0.0).astype(BF16) for hh in range(SGU_HEADS)]

    qi = lax.broadcasted_iota(jnp.int32, (WINDOW, 2 * WINDOW), 0)
    kj = lax.broadcasted_iota(jnp.int32, (WINDOW, 2 * WINDOW), 1)
    band = (kj > qi) & (kj <= qi + WINDOW)
    lo_half_w = lax.broadcasted_iota(jnp.int32, (WINDOW, LANES), 1) < HEAD_DIM

    def chunk_body(n, carry):
        r0 = pl.multiple_of(n * WINDOW, WINDOW)
        first = jnp.logical_and(s_idx == 0, n == 0)
        min_key = jnp.where(first, WINDOW, 0)
        mask = band & (kj >= min_key)
        for col in range(ATT_WIDTH // LANES):
            g = col // (ATT_Q_HEADS // ATT_KV_HEADS // 2)
            kd = kd_scr[g, pl.ds(r0, 2 * WINDOW), :]
            vd = vd_scr[g, pl.ds(r0, 2 * WINDOW), :]
            outs = []
            for side in range(2):
                hh = 2 * col + side
                qm = q_scr[side, pl.ds(r0, WINDOW), col * LANES:(col + 1) * LANES]
                s = lax.dot_general(qm, kd, (((1,), (1,)), ((), ())),
                                    preferred_element_type=F32)
                s = jnp.where(mask, s, NEG_INF)
                sink = sinks_ref[hh]
                m = jnp.maximum(jnp.max(s, axis=1, keepdims=True), sink)
                p = jnp.exp(s - m)
                den = jnp.sum(p, axis=1, keepdims=True) + jnp.exp(sink - m)
                o = jnp.dot(p.astype(BF16), vd, preferred_element_type=F32)
                outs.append(o / den)
            att_scr[pl.ds(r0, WINDOW), col * LANES:(col + 1) * LANES] = jnp.where(
                lo_half_w, outs[0], outs[1])
        for c in range(SGU_WIDTH // LANES):
            sl = slice(c * LANES, (c + 1) * LANES)
            mixed = (jnp.dot(w_mix[2 * c], vn_scr[0, pl.ds(r0, SGU_CHUNK), sl],
                             preferred_element_type=F32)
                     + jnp.dot(w_mix[2 * c + 1], vn_scr[1, pl.ds(r0, SGU_CHUNK), sl],
                               preferred_element_type=F32))
            sgu_scr[pl.ds(r0, SGU_CHUNK), sl] = (
                u_scr[pl.ds(r0, SGU_CHUNK), sl] * (mixed + bias_ref[:, sl]))
        return carry

    lax.fori_loop(0, n_chunks, chunk_body, 0)

    kd_scr[:, 0:WINDOW, :] = kd_scr[:, ts:ts + WINDOW, :]
    vd_scr[:, 0:WINDOW, :] = vd_scr[:, ts:ts + WINDOW, :]

    ya = (_rms(att_scr[...]) * gatt_ref[...]).astype(BF16)
    ys = (_rms(sgu_scr[...]) * gsgu_ref[...]).astype(BF16)
    y = (jnp.dot(ya, wout_ref[0:ATT_WIDTH, :], preferred_element_type=F32)
         + jnp.dot(ys, wout_ref[ATT_WIDTH:ATT_WIDTH + SGU_WIDTH, :], preferred_element_type=F32))
    o_ref[...] = x + gt1 * (_rms(y) * gpost_ref[...])


def _rope_tables(seq):
    half = HEAD_DIM // 2
    inv = ROPE_THETA ** (-jnp.arange(half, dtype=F32) / half)
    ang = jnp.arange(seq, dtype=F32)[:, None] * inv[None, :]
    d = jnp.arange(LANES) % HEAD_DIM
    cos = jnp.cos(ang)[:, d % half]
    sin = jnp.sin(ang)[:, d % half]
    first = (d < half)[None, :]
    return cos, jnp.where(first, -sin, 0.0), jnp.where(first, 0.0, sin)


def _mixer(x, mod3, g_pre, g_post, w_in, w_s, b_s, ln_g, ln_b, sinks, g_att, g_sgu, w_out, ts):
    bsz, seq, d = x.shape
    cos, sina, sinb = _rope_tables(seq)
    bias_full = jnp.repeat(b_s.T, HEAD_DIM, axis=1)
    row = lambda a: a.reshape(1, -1)
    full = lambda shape: pl.BlockSpec(shape, lambda b, s: (0,) * len(shape))
    return pl.pallas_call(
        functools.partial(_mixer_kernel, ts=ts),
        out_shape=jax.ShapeDtypeStruct((bsz, seq, d), F32),
        grid=(bsz, seq // ts),
        in_specs=[
            pl.BlockSpec(memory_space=pltpu.SMEM),
            pl.BlockSpec((None, ts, d), lambda b, s: (b, s, 0)),
            pl.BlockSpec((None, 6, d), lambda b, s: (b, 0, 0)),
            full((1, d)), full((1, d)),
            full((d, IN_COLS)), full((ATT_WIDTH + SGU_WIDTH, d)),
            pl.BlockSpec((ts, LANES), lambda b, s: (s, 0)),
            pl.BlockSpec((ts, LANES), lambda b, s: (s, 0)),
            pl.BlockSpec((ts, LANES), lambda b, s: (s, 0)),
            full((SGU_HEADS, SGU_CHUNK, SGU_CHUNK)),
            full((SGU_CHUNK, SGU_WIDTH)),
            full((1, SGU_WIDTH)), full((1, SGU_WIDTH)),
            full((1, ATT_WIDTH)), full((1, SGU_WIDTH)),
        ],
        out_specs=pl.BlockSpec((None, ts, d), lambda b, s: (b, s, 0)),
        scratch_shapes=[
            pltpu.VMEM((2, ts, ATT_WIDTH), BF16),
            pltpu.VMEM((2, ts + WINDOW, LANES), BF16),
            pltpu.VMEM((2, ts + WINDOW, LANES), BF16),
            pltpu.VMEM((ts, SGU_WIDTH), F32),
            pltpu.VMEM((2, ts, SGU_WIDTH), BF16),
            pltpu.VMEM((ts, ATT_WIDTH), F32),
            pltpu.VMEM((ts, SGU_WIDTH), F32),
        ],
        compiler_params=pltpu.CompilerParams(
            dimension_semantics=("arbitrary", "arbitrary"), vmem_limit_bytes=VMEM_LIMIT_BYTES),
        name="token_mixer",
    )(sinks, x, mod3, row(g_pre), row(g_post), w_in.astype(BF16), w_out.astype(BF16),
      cos, sina, sinb, w_s, bias_full, row(ln_g), row(ln_b), row(g_att), row(g_sgu))


def _top_rows(a, k, with_rank=False):
    rows = []
    x = a
    rank = jnp.full(a.shape, float(k), F32) if with_rank else None
    for r in range(k):
        m = jnp.max(x, axis=0, keepdims=True)
        rows.append(m)
        if with_rank or r + 1 < k:
            hit = x == m
            if with_rank:
                rank = jnp.where(hit, float(r), rank)
            x = jnp.where(hit, NEG_INF, x)
    return (rows, rank) if with_rank else rows


def _pack_rows(rows):
    n = len(rows)
    rid = lax.broadcasted_iota(jnp.int32, (n, LANES), 0)
    out = jnp.broadcast_to(rows[0], (n, LANES))
    for r in range(1, n):
        out = jnp.where(rid == r, rows[r], out)
    return out


def _route_lane_group(a1, a2):
    k = PEER_TOPK
    t1 = _top_rows(a1, k)
    t2, rank2 = _top_rows(a2, k, with_rank=True)
    t2p = _pack_rows(t2)
    t1_hi = _pack_rows(t1[k // 2:])
    rid8 = lax.broadcasted_iota(jnp.int32, (8, LANES), 0)
    pieces = [t1[0] + t2p]
    for a in range(1, k // 2):
        nb = k // (a + 1)
        cnd = t1[a] + t2p[0:8]
        if nb < 8:
            cnd = jnp.where(rid8 < nb, cnd, NEG_INF)
        pieces.append(cnd)
    pieces.append(t1_hi + t2[0])
    cand = jnp.concatenate(pieces, axis=0)
    top = _top_rows(cand, k)
    thr = top[k - 1]
    z = jnp.sum(jnp.where(cand >= thr, jnp.exp(cand - top[0]), 0.0), axis=0, keepdims=True)
    count1 = jnp.zeros(a1.shape, F32)
    for b in range(k):
        count1 = count1 + jnp.where(a1 + t2[b] >= thr, 1.0, 0.0)
    e1 = jnp.exp(a1 - t1[0]) / z
    e2 = jnp.exp(a2 - t2[0])
    return count1, e1, rank2, e2


BF16_ROWS = 16


def _dup_bf16_bits(v):
    b = pltpu.bitcast(v.astype(BF16).astype(F32), jnp.uint32)
    return b | (b >> 16)


LG_PER_SLAB = 2


def _gate_slab(chunk, k, at_ref, h_ref, n1_scr, e1_scr, r2_scr, e2_scr, n_i):
    n_sub = PEER_N_KEYS // BF16_ROWS
    zero = jnp.zeros((), BF16)

    def row_bf16(tab, hh, lg, i):
        row = jnp.broadcast_to(tab[hh, lg, pl.ds(i, 1), :], (BF16_ROWS // 2, LANES))
        return pltpu.bitcast(row, BF16)

    for l in range(LG_PER_SLAB):
        lg = k * LG_PER_SLAB + l
        sl = slice(l * LANES, (l + 1) * LANES)
        gate = [[None] * n_sub for _ in range(n_i)]
        for hh in range(PEER_HEADS):
            n1 = [row_bf16(n1_scr, hh, lg, chunk * n_i + ii) for ii in range(n_i)]
            e1 = [row_bf16(e1_scr, hh, lg, chunk * n_i + ii) for ii in range(n_i)]
            for r in range(n_sub):
                rs = slice(r * BF16_ROWS, (r + 1) * BF16_ROWS)
                r2 = r2_scr[hh, lg, rs, :]
                e2 = e2_scr[hh, lg, rs, :]
                for ii in range(n_i):
                    w = jnp.where(r2 < n1[ii], e2, zero) * e1[ii]
                    gate[ii][r] = w if hh == 0 else gate[ii][r] + w
        for ii in range(n_i):
            for r in range(n_sub):
                rows = slice(ii * PEER_N_KEYS + r * BF16_ROWS, ii * PEER_N_KEYS + (r + 1) * BF16_ROWS)
                h_ref[k, rows, sl] = _gelu(at_ref[k, rows, sl]).astype(BF16) * gate[ii][r]


def _pipe_chunk(chunk, at_cur, h_cur, ed_nxt, at_nxt, eut_prv, h_prv, xt_scr, acc_scr, tabs,
                n_slab):
    def slab_body(k, carry):
        _gate_slab(chunk, k, at_cur, h_cur, *tabs)
        at_nxt[k] = jnp.dot(ed_nxt[...], xt_scr[k], preferred_element_type=F32)
        acc_scr[k] += jnp.dot(eut_prv[...], h_prv[k], preferred_element_type=F32)
        return carry

    lax.fori_loop(0, n_slab, slab_body, 0)


def _peer_kernel(x_ref, mod_ref, gpre_ref, gpost_ref, wqt_ref, sk_ref,
                 ed0_ref, edb_ref, edn_ref, eutp_ref, euta_ref, eutl_ref, o_ref,
                 xt_scr, acc_scr, s_scr, n1_scr, e1_scr, r2_scr, e2_scr,
                 ata_scr, atb_scr, ha_scr, hb_scr, *, tb, ec):
    j = pl.program_id(1)
    n_lg = tb // LANES
    n_slab = n_lg // LG_PER_SLAB
    slab = LG_PER_SLAB * LANES
    n_i = ec // PEER_N_KEYS

    @pl.when(j == 0)
    def _prologue():
        x = x_ref[...]
        sh2 = mod_ref[3:4, :]
        sc2 = mod_ref[4:5, :]
        h = _rms(x) * gpre_ref[...] * (1.0 + sc2) + sh2
        ht = h.T.astype(BF16)
        for k in range(n_slab):
            xt_scr[k] = ht[:, k * slab:(k + 1) * slab]
        acc_scr[...] = jnp.zeros_like(acc_scr)

        def head_body(hh, carry):
            r0 = pl.multiple_of(hh * PEER_QDIM, PEER_QDIM)
            half = PEER_QDIM // 2
            for k in range(n_slab):
                qt = jnp.dot(wqt_ref[pl.ds(r0, PEER_QDIM), :], xt_scr[k],
                             preferred_element_type=F32)
                s1 = jnp.dot(sk_ref[2 * hh], qt[0:half].astype(BF16), preferred_element_type=F32)
                s2 = jnp.dot(sk_ref[2 * hh + 1], qt[half:].astype(BF16),
                             preferred_element_type=F32)
                for l in range(LG_PER_SLAB):
                    s_scr[0, k * LG_PER_SLAB + l] = s1[:, l * LANES:(l + 1) * LANES]
                    s_scr[1, k * LG_PER_SLAB + l] = s2[:, l * LANES:(l + 1) * LANES]

            def lg_body(lg, c2):
                count1, e1, rank2, e2 = _route_lane_group(s_scr[0, lg], s_scr[1, lg])
                n1_scr[hh, lg] = _dup_bf16_bits(count1)
                e1_scr[hh, lg] = _dup_bf16_bits(e1)
                r2_scr[hh, lg] = rank2.astype(BF16)
                e2_scr[hh, lg] = e2.astype(BF16)
                return c2

            lax.fori_loop(0, n_lg, lg_body, 0)
            return carry

        lax.fori_loop(0, PEER_HEADS, head_body, 0)
        for k in range(n_slab):
            ata_scr[k] = jnp.dot(ed0_ref[...], xt_scr[k], preferred_element_type=F32)
        hb_scr[...] = jnp.zeros_like(hb_scr)

    tabs = (n1_scr, e1_scr, r2_scr, e2_scr, n_i)
    _pipe_chunk(2 * j, ata_scr, ha_scr, edb_ref, atb_scr, eutp_ref, hb_scr,
                xt_scr, acc_scr, tabs, n_slab)
    _pipe_chunk(2 * j + 1, atb_scr, hb_scr, edn_ref, ata_scr, euta_ref, ha_scr,
                xt_scr, acc_scr, tabs, n_slab)

    @pl.when(j == pl.num_programs(1) - 1)
    def _epilogue():
        y = jnp.concatenate(
            [(acc_scr[k] + jnp.dot(eutl_ref[...], hb_scr[k], preferred_element_type=F32)).T
             for k in range(n_slab)], axis=0)
        gt2 = mod_ref[5:6, :]
        o_ref[...] = x_ref[...] + gt2 * (_rms(y) * gpost_ref[...])


def _peer(x1, mod3, g_pre, g_post, w_query, sub_keys, expert_down, expert_up, seq, tb, ec):
    t, d = x1.shape
    ne = expert_down.shape[0]
    n_lg = tb // LANES
    wqt = w_query.T.astype(BF16)
    sk = sub_keys.reshape(PEER_HEADS * 2, PEER_N_KEYS, PEER_QDIM // 2).astype(BF16)
    ed = expert_down.astype(BF16)
    eut = expert_up.T.astype(BF16)
    row = lambda a: a.reshape(1, -1)
    full = lambda shape: pl.BlockSpec(shape, lambda i, j: (0,) * len(shape))
    tiles_per_seq = seq // tb
    n_chunks = ne // ec
    last = n_chunks - 1
    table = lambda dt: pltpu.VMEM((PEER_HEADS, n_lg, PEER_N_KEYS, LANES), dt)
    slabbed = lambda rows, dt: pltpu.VMEM(
        (n_lg // LG_PER_SLAB, rows, LG_PER_SLAB * LANES), dt)
    return pl.pallas_call(
        functools.partial(_peer_kernel, tb=tb, ec=ec),
        out_shape=jax.ShapeDtypeStruct((t, d), F32),
        grid=(t // tb, n_chunks // 2),
        in_specs=[
            pl.BlockSpec((tb, d), lambda i, j: (i, 0)),
            pl.BlockSpec((None, 6, d), lambda i, j: (i // tiles_per_seq, 0, 0)),
            full((1, d)), full((1, d)),
            full((PEER_HEADS * PEER_QDIM, d)),
            full((PEER_HEADS * 2, PEER_N_KEYS, PEER_QDIM // 2)),
            pl.BlockSpec((ec, d), lambda i, j: (0, 0)),
            pl.BlockSpec((ec, d), lambda i, j: (2 * j + 1, 0)),
            pl.BlockSpec((ec, d), lambda i, j: (jnp.minimum(2 * j + 2, last), 0)),
            pl.BlockSpec((d, ec), lambda i, j: (0, jnp.maximum(2 * j - 1, 0))),
            pl.BlockSpec((d, ec), lambda i, j: (0, 2 * j)),
            pl.BlockSpec((d, ec), lambda i, j: (0, last)),
        ],
        out_specs=pl.BlockSpec((tb, d), lambda i, j: (i, 0)),
        scratch_shapes=[
            slabbed(d, BF16),
            slabbed(d, F32),
            pltpu.VMEM((2, n_lg, PEER_N_KEYS, LANES), F32),
            table(jnp.uint32), table(jnp.uint32), table(BF16), table(BF16),
            slabbed(ec, F32), slabbed(ec, F32),
            slabbed(ec, BF16), slabbed(ec, BF16),
        ],
        compiler_params=pltpu.CompilerParams(
            dimension_semantics=("arbitrary", "arbitrary"), vmem_limit_bytes=VMEM_LIMIT_BYTES),
        name="peer_dense",
    )(x1, mod3, row(g_pre), row(g_post), wqt, sk, ed, ed, ed, eut, eut, eut)


def _pick_tile(n, target):
    t = min(n, target)
    while n % t:
        t -= LANES
    return t


def kernel(x, c, w_ada, b_ada, g_pre_mix, g_post_mix, g_pre_ffn, g_post_ffn, w_in, w_s, b_s,
           sgu_ln_g, sgu_ln_b, sinks, g_att_out, g_sgu_out, w_out, w_query, sub_keys,
           expert_down, expert_up):
    bsz, seq, d = x.shape
    ts = _pick_tile(seq, 512)
    tb = _pick_tile(seq, 512)
    ec = _pick_tile(expert_down.shape[1], 512)
    for l in range(w_ada.shape[0]):
        mod3 = _ada_modulation(c, w_ada[l], b_ada[l]).reshape(bsz, 6, d)
        x = _mixer(x, mod3, g_pre_mix[l], g_post_mix[l], w_in[l], w_s[l], b_s[l],
                   sgu_ln_g[l].reshape(-1), sgu_ln_b[l].reshape(-1), sinks[l],
                   g_att_out[l], g_sgu_out[l], w_out[l], ts)
        y = _peer(x.reshape(bsz * seq, d), mod3, g_pre_ffn[l], g_post_ffn[l], w_query[l],
                  sub_keys[l], expert_down[l], expert_up[l], seq, tb, ec)
        x = y.reshape(bsz, seq, d)
    return x
```

```python
import functools

import jax
import jax.numpy as jnp
from jax import lax
from jax.experimental import pallas as pl
from jax.experimental.pallas import tpu as pltpu

HEAD_DIM = 64
ATT_Q_HEADS = 8
ATT_KV_HEADS = 2
WINDOW = 128
ROPE_THETA = 10000.0
SGU_HEADS = 8
SGU_CHUNK = 128
ATT_WIDTH = ATT_Q_HEADS * HEAD_DIM
KV_WIDTH = ATT_KV_HEADS * HEAD_DIM
SGU_WIDTH = SGU_HEADS * HEAD_DIM
IN_COLS = ATT_WIDTH + 2 * KV_WIDTH + 2 * SGU_WIDTH
PEER_HEADS = 8
PEER_N_KEYS = 128
PEER_TOPK = 16
PEER_QDIM = 256
EPS = 1e-6

LANES = 128
VMEM_LIMIT_BYTES = 56 * 1024 * 1024

F32 = jnp.float32
BF16 = jnp.bfloat16
NEG_INF = float("-inf")


def _rms(x):
    return x * lax.rsqrt(jnp.mean(x * x, axis=-1, keepdims=True) + EPS)


def _gelu(t):
    return 0.5 * t * (1.0 + lax.erf(t * 0.7071067811865476))


def _split_bf16(t):
    hi = t.astype(BF16)
    lo = (t - hi.astype(F32)).astype(BF16)
    return hi, lo


def _ada_kernel(c_ref, w_ref, b_ref, o_ref):
    c = c_ref[...]
    c_act = c * (1.0 / (1.0 + jnp.exp(-c)))
    c_hi, c_lo = _split_bf16(c_act)
    w_hi, w_lo = _split_bf16(w_ref[...])
    acc = jnp.dot(c_hi, w_hi, preferred_element_type=F32)
    acc += jnp.dot(c_lo, w_hi, preferred_element_type=F32)
    acc += jnp.dot(c_hi, w_lo, preferred_element_type=F32)
    o_ref[...] = acc + b_ref[...]


def _ada_modulation(c, w_ada, b_ada):
    bsz, d = c.shape
    n = w_ada.shape[1]
    bn = 1024
    return pl.pallas_call(
        _ada_kernel,
        out_shape=jax.ShapeDtypeStruct((bsz, n), F32),
        grid=(n // bn,),
        in_specs=[
            pl.BlockSpec((bsz, d), lambda j: (0, 0)),
            pl.BlockSpec((d, bn), lambda j: (0, j)),
            pl.BlockSpec((1, bn), lambda j: (0, j)),
        ],
        out_specs=pl.BlockSpec((bsz, bn), lambda j: (0, j)),
        compiler_params=pltpu.CompilerParams(
            dimension_semantics=("arbitrary",), vmem_limit_bytes=VMEM_LIMIT_BYTES),
        name="ada_modulation",
    )(c, w_ada, b_ada.reshape(1, n))


def _mixer_kernel(sinks_ref, x_ref, mod_ref, gpre_ref, gpost_ref, win_ref, wout_ref,
                  cos_ref, sina_ref, sinb_ref, ws_ref, bias_ref, lng_ref, lnb_ref,
                  gatt_ref, gsgu_ref, o_ref,
                  q_scr, kd_scr, vd_scr, u_scr, vn_scr, att_scr, sgu_scr, *, ts):
    s_idx = pl.program_id(1)
    n_chunks = ts // WINDOW

    x = x_ref[...]
    sh1 = mod_ref[0:1, :]
    sc1 = mod_ref[1:2, :]
    gt1 = mod_ref[2:3, :]
    h = _rms(x) * gpre_ref[...] * (1.0 + sc1) + sh1
    proj = jnp.dot(h.astype(BF16), win_ref[...], preferred_element_type=F32)

    lane = lax.broadcasted_iota(jnp.int32, (ts, LANES), 1)
    lo_half = lane < HEAD_DIM

    cos = cos_ref[...]
    sina = sina_ref[...]
    sinb = sinb_ref[...]

    def rope(t):
        return (t * cos + pltpu.roll(t, LANES - HEAD_DIM // 2, 1) * sina
                + pltpu.roll(t, HEAD_DIM // 2, 1) * sinb)

    for c in range(ATT_WIDTH // LANES):
        qc = rope(proj[:, c * LANES:(c + 1) * LANES]) * (HEAD_DIM ** -0.5)
        q_scr[0, :, c * LANES:(c + 1) * LANES] = jnp.where(lo_half, qc, 0.0).astype(BF16)
        q_scr[1, :, c * LANES:(c + 1) * LANES] = jnp.where(lo_half, 0.0, qc).astype(BF16)

    k = rope(proj[:, ATT_WIDTH:ATT_WIDTH + KV_WIDTH])
    v = proj[:, ATT_WIDTH + KV_WIDTH:ATT_WIDTH + 2 * KV_WIDTH]
    k_sw = pltpu.roll(k, HEAD_DIM, 1)
    v_sw = pltpu.roll(v, HEAD_DIM, 1)

    @pl.when(s_idx == 0)
    def _():
        kd_scr[:, 0:WINDOW, :] = jnp.zeros((2, WINDOW, LANES), BF16)
        vd_scr[:, 0:WINDOW, :] = jnp.zeros((2, WINDOW, LANES), BF16)

    kd_scr[0, WINDOW:WINDOW + ts, :] = jnp.where(lo_half, k, k_sw).astype(BF16)
    kd_scr[1, WINDOW:WINDOW + ts, :] = jnp.where(lo_half, k_sw, k).astype(BF16)
    vd_scr[0, WINDOW:WINDOW + ts, :] = jnp.where(lo_half, v, v_sw).astype(BF16)
    vd_scr[1, WINDOW:WINDOW + ts, :] = jnp.where(lo_half, v_sw, v).astype(BF16)

    u_scr[...] = _gelu(proj[:, ATT_WIDTH + 2 * KV_WIDTH:ATT_WIDTH + 2 * KV_WIDTH + SGU_WIDTH])
    vs = _gelu(proj[:, ATT_WIDTH + 2 * KV_WIDTH + SGU_WIDTH:IN_COLS])
    ri = lax.broadcasted_iota(jnp.int32, (LANES, LANES), 0)
    ci = lax.broadcasted_iota(jnp.int32, (LANES, LANES), 1)
    avg = jnp.where((ri // HEAD_DIM) == (ci // HEAD_DIM), 1.0 / HEAD_DIM, 0.0).astype(BF16)

    def group_mean(t):
        hi, lo = _split_bf16(t)
        return (jnp.dot(hi, avg, preferred_element_type=F32)
                + jnp.dot(lo, avg, preferred_element_type=F32))

    for c in range(SGU_WIDTH // LANES):
        sl = slice(c * LANES, (c + 1) * LANES)
        vc = vs[:, sl]
        d = vc - group_mean(vc)
        vn = d * lax.rsqrt(group_mean(d * d) + EPS) * lng_ref[:, sl] + lnb_ref[:, sl]
        vn_scr[0, :, sl] = jnp.where(lo_half, vn, 0.0).astype(BF16)
        vn_scr[1, :, sl] = jnp.where(lo_half, 0.0, vn).astype(BF16)

    tril = ci <= ri
    w_mix = [jnp.where(tril, ws_ref[hh], 0.0).astype(BF16) for hh in range(SGU_HEADS)]

    qi = lax.broadcasted_iota(jnp.int32, (WINDOW, 2 * WINDOW), 0)
    kj = lax.broadcasted_iota(jnp.int32, (WINDOW, 2 * WINDOW), 1)
    band = (kj > qi) & (kj <= qi + WINDOW)
    lo_half_w = lax.broadcasted_iota(jnp.int32, (WINDOW, LANES), 1) < HEAD_DIM

    def chunk_body(n, carry):
        r0 = pl.multiple_of(n * WINDOW, WINDOW)
        first = jnp.logical_and(s_idx == 0, n == 0)
        min_key = jnp.where(first, WINDOW, 0)
        mask = band & (kj >= min_key)
        for col in range(ATT_WIDTH // LANES):
            g = col // (ATT_Q_HEADS // ATT_KV_HEADS // 2)
            kd = kd_scr[g, pl.ds(r0, 2 * WINDOW), :]
            vd = vd_scr[g, pl.ds(r0, 2 * WINDOW), :]
            outs = []
            for side in range(2):
                hh = 2 * col + side
                qm = q_scr[side, pl.ds(r0, WINDOW), col * LANES:(col + 1) * LANES]
                s = lax.dot_general(qm, kd, (((1,), (1,)), ((), ())),
                                    preferred_element_type=F32)
                s = jnp.where(mask, s, NEG_INF)
                sink = sinks_ref[hh]
                m = jnp.maximum(jnp.max(s, axis=1, keepdims=True), sink)
                p = jnp.exp(s - m)
                den = jnp.sum(p, axis=1, keepdims=True) + jnp.exp(sink - m)
                o = jnp.dot(p.astype(BF16), vd, preferred_element_type=F32)
                outs.append(o / den)
            att_scr[pl.ds(r0, WINDOW), col * LANES:(col + 1) * LANES] = jnp.where(
                lo_half_w, outs[0], outs[1])
        for c in range(SGU_WIDTH // LANES):
            sl = slice(c * LANES, (c + 1) * LANES)
            mixed = (jnp.dot(w_mix[2 * c], vn_scr[0, pl.ds(r0, SGU_CHUNK), sl],
                             preferred_element_type=F32)
                     + jnp.dot(w_mix[2 * c + 1], vn_scr[1, pl.ds(r0, SGU_CHUNK), sl],
                               preferred_element_type=F32))
            sgu_scr[pl.ds(r0, SGU_CHUNK), sl] = (
                u_scr[pl.ds(r0, SGU_CHUNK), sl] * (mixed + bias_ref[:, sl]))
        return carry

    lax.fori_loop(0, n_chunks, chunk_body, 0)

    kd_scr[:, 0:WINDOW, :] = kd_scr[:, ts:ts + WINDOW, :]
    vd_scr[:, 0:WINDOW, :] = vd_scr[:, ts:ts + WINDOW, :]

    ya = (_rms(att_scr[...]) * gatt_ref[...]).astype(BF16)
    ys = (_rms(sgu_scr[...]) * gsgu_ref[...]).astype(BF16)
    y = (jnp.dot(ya, wout_ref[0:ATT_WIDTH, :], preferred_element_type=F32)
         + jnp.dot(ys, wout_ref[ATT_WIDTH:ATT_WIDTH + SGU_WIDTH, :], preferred_element_type=F32))
    o_ref[...] = x + gt1 * (_rms(y) * gpost_ref[...])


def _rope_tables(seq):
    half = HEAD_DIM // 2
    inv = ROPE_THETA ** (-jnp.arange(half, dtype=F32) / half)
    ang = jnp.arange(seq, dtype=F32)[:, None] * inv[None, :]
    d = jnp.arange(LANES) % HEAD_DIM
    cos = jnp.cos(ang)[:, d % half]
    sin = jnp.sin(ang)[:, d % half]
    first = (d < half)[None, :]
    return cos, jnp.where(first, -sin, 0.0), jnp.where(first, 0.0, sin)


def _mixer(x, mod3, g_pre, g_post, w_in, w_s, b_s, ln_g, ln_b, sinks, g_att, g_sgu, w_out, ts):
    bsz, seq, d = x.shape
    cos, sina, sinb = _rope_tables(seq)
    bias_full = jnp.repeat(b_s.T, HEAD_DIM, axis=1)
    row = lambda a: a.reshape(1, -1)
    full = lambda shape: pl.BlockSpec(shape, lambda b, s: (0,) * len(shape))
    return pl.pallas_call(
        functools.partial(_mixer_kernel, ts=ts),
        out_shape=jax.ShapeDtypeStruct((bsz, seq, d), F32),
        grid=(bsz, seq // ts),
        in_specs=[
            pl.BlockSpec(memory_space=pltpu.SMEM),
            pl.BlockSpec((None, ts, d), lambda b, s: (b, s, 0)),
            pl.BlockSpec((None, 6, d), lambda b, s: (b, 0, 0)),
            full((1, d)), full((1, d)),
            full((d, IN_COLS)), full((ATT_WIDTH + SGU_WIDTH, d)),
            pl.BlockSpec((ts, LANES), lambda b, s: (s, 0)),
            pl.BlockSpec((ts, LANES), lambda b, s: (s, 0)),
            pl.BlockSpec((ts, LANES), lambda b, s: (s, 0)),
            full((SGU_HEADS, SGU_CHUNK, SGU_CHUNK)),
            full((SGU_CHUNK, SGU_WIDTH)),
            full((1, SGU_WIDTH)), full((1, SGU_WIDTH)),
            full((1, ATT_WIDTH)), full((1, SGU_WIDTH)),
        ],
        out_specs=pl.BlockSpec((None, ts, d), lambda b, s: (b, s, 0)),
        scratch_shapes=[
            pltpu.VMEM((2, ts, ATT_WIDTH), BF16),
            pltpu.VMEM((2, ts + WINDOW, LANES), BF16),
            pltpu.VMEM((2, ts + WINDOW, LANES), BF16),
            pltpu.VMEM((ts, SGU_WIDTH), F32),
            pltpu.VMEM((2, ts, SGU_WIDTH), BF16),
            pltpu.VMEM((ts, ATT_WIDTH), F32),
            pltpu.VMEM((ts, SGU_WIDTH), F32),
        ],
        compiler_params=pltpu.CompilerParams(
            dimension_semantics=("arbitrary", "arbitrary"), vmem_limit_bytes=VMEM_LIMIT_BYTES),
        name="token_mixer",
    )(sinks, x, mod3, row(g_pre), row(g_post), w_in.astype(BF16), w_out.astype(BF16),
      cos, sina, sinb, w_s, bias_full, row(ln_g), row(ln_b), row(g_att), row(g_sgu))


def _top_rows(a, k, with_rank=False):
    rows = []
    x = a
    rank = jnp.full(a.shape, float(k), F32) if with_rank else None
    for r in range(k):
        m = jnp.max(x, axis=0, keepdims=True)
        rows.append(m)
        if with_rank or r + 1 < k:
            hit = x == m
            if with_rank:
                rank = jnp.where(hit, float(r), rank)
            x = jnp.where(hit, NEG_INF, x)
    return (rows, rank) if with_rank else rows


def _pack_rows(rows):
    n = len(rows)
    rid = lax.broadcasted_iota(jnp.int32, (n, LANES), 0)
    out = jnp.broadcast_to(rows[0], (n, LANES))
    for r in range(1, n):
        out = jnp.where(rid == r, rows[r], out)
    return out


def _route_lane_group(a1, a2):
    k = PEER_TOPK
    t1 = _top_rows(a1, k)
    t2, rank2 = _top_rows(a2, k, with_rank=True)
    t2p = _pack_rows(t2)
    t1_hi = _pack_rows(t1[k // 2:])
    rid8 = lax.broadcasted_iota(jnp.int32, (8, LANES), 0)
    pieces = [t1[0] + t2p]
    for a in range(1, k // 2):
        nb = k // (a + 1)
        cnd = t1[a] + t2p[0:8]
        if nb < 8:
            cnd = jnp.where(rid8 < nb, cnd, NEG_INF)
        pieces.append(cnd)
    pieces.append(t1_hi + t2[0])
    cand = jnp.concatenate(pieces, axis=0)
    top = _top_rows(cand, k)
    thr = top[k - 1]
    z = jnp.sum(jnp.where(cand >= thr, jnp.exp(cand - top[0]), 0.0), axis=0, keepdims=True)
    count1 = jnp.zeros(a1.shape, F32)
    for b in range(k):
        count1 = count1 + jnp.where(a1 + t2[b] >= thr, 1.0, 0.0)
    e1 = jnp.exp(a1 - t1[0]) / z
    e2 = jnp.exp(a2 - t2[0])
    return count1, e1, rank2, e2


BF16_ROWS = 16


def _dup_bf16_bits(v):
    b = pltpu.bitcast(v.astype(BF16).astype(F32), jnp.uint32)
    return b | (b >> 16)


LG_PER_SLAB = 2


def _gate_slab(chunk, k, at_ref, h_ref, n1_scr, e1_scr, r2_scr, e2_scr, n_i):
    n_sub = PEER_N_KEYS // BF16_ROWS
    zero = jnp.zeros((), BF16)

    def row_bf16(tab, hh, lg, i):
        row = jnp.broadcast_to(tab[hh, lg, pl.ds(i, 1), :], (BF16_ROWS // 2, LANES))
        return pltpu.bitcast(row, BF16)

    for l in range(LG_PER_SLAB):
        lg = k * LG_PER_SLAB + l
        sl = slice(l * LANES, (l + 1) * LANES)
        gate = [[None] * n_sub for _ in range(n_i)]
        for hh in range(PEER_HEADS):
            n1 = [row_bf16(n1_scr, hh, lg, chunk * n_i + ii) for ii in range(n_i)]
            e1 = [row_bf16(e1_scr, hh, lg, chunk * n_i + ii) for ii in range(n_i)]
            for r in range(n_sub):
                rs = slice(r * BF16_ROWS, (r + 1) * BF16_ROWS)
                r2 = r2_scr[hh, lg, rs, :]
                e2 = e2_scr[hh, lg, rs, :]
                for ii in range(n_i):
                    w = jnp.where(r2 < n1[ii], e2, zero) * e1[ii]
                    gate[ii][r] = w if hh == 0 else gate[ii][r] + w
        for ii in range(n_i):
            for r in range(n_sub):
                rows = slice(ii * PEER_N_KEYS + r * BF16_ROWS, ii * PEER_N_KEYS + (r + 1) * BF16_ROWS)
                h_ref[k, rows, sl] = _gelu(at_ref[k, rows, sl].astype(BF16)) * gate[ii][r]


def _pipe_chunk(chunk, at_cur, h_cur, ed_nxt, at_nxt, eut_prv, h_prv, xt_scr, acc_scr, tabs,
                n_slab):
    def slab_body(k, carry):
        _gate_slab(chunk, k, at_cur, h_cur, *tabs)
        at_nxt[k] = jnp.dot(ed_nxt[...], xt_scr[k], preferred_element_type=F32)
        acc_scr[k] += jnp.dot(eut_prv[...], h_prv[k], preferred_element_type=F32)
        return carry

    lax.fori_loop(0, n_slab, slab_body, 0)


def _peer_kernel(x_ref, mod_ref, gpre_ref, gpost_ref, wqt_ref, sk_ref,
                 ed0_ref, edb_ref, edn_ref, eutp_ref, euta_ref, eutl_ref, o_ref,
                 xt_scr, acc_scr, s_scr, n1_scr, e1_scr, r2_scr, e2_scr,
                 ata_scr, atb_scr, ha_scr, hb_scr, *, tb, ec):
    j = pl.program_id(1)
    n_lg = tb // LANES
    n_slab = n_lg // LG_PER_SLAB
    slab = LG_PER_SLAB * LANES
    n_i = ec // PEER_N_KEYS

    @pl.when(j == 0)
    def _prologue():
        x = x_ref[...]
        sh2 = mod_ref[3:4, :]
        sc2 = mod_ref[4:5, :]
        h = _rms(x) * gpre_ref[...] * (1.0 + sc2) + sh2
        ht = h.T.astype(BF16)
        for k in range(n_slab):
            xt_scr[k] = ht[:, k * slab:(k + 1) * slab]
        acc_scr[...] = jnp.zeros_like(acc_scr)

        def head_body(hh, carry):
            r0 = pl.multiple_of(hh * PEER_QDIM, PEER_QDIM)
            half = PEER_QDIM // 2
            for k in range(n_slab):
                qt = jnp.dot(wqt_ref[pl.ds(r0, PEER_QDIM), :], xt_scr[k],
                             preferred_element_type=F32)
                s1 = jnp.dot(sk_ref[2 * hh], qt[0:half].astype(BF16), preferred_element_type=F32)
                s2 = jnp.dot(sk_ref[2 * hh + 1], qt[half:].astype(BF16),
                             preferred_element_type=F32)
                for l in range(LG_PER_SLAB):
                    s_scr[0, k * LG_PER_SLAB + l] = s1[:, l * LANES:(l + 1) * LANES]
                    s_scr[1, k * LG_PER_SLAB + l] = s2[:, l * LANES:(l + 1) * LANES]

            def lg_body(lg, c2):
                count1, e1, rank2, e2 = _route_lane_group(s_scr[0, lg], s_scr[1, lg])
                n1_scr[hh, lg] = _dup_bf16_bits(count1)
                e1_scr[hh, lg] = _dup_bf16_bits(e1)
                r2_scr[hh, lg] = rank2.astype(BF16)
                e2_scr[hh, lg] = e2.astype(BF16)
                return c2

            lax.fori_loop(0, n_lg, lg_body, 0)
            return carry

        lax.fori_loop(0, PEER_HEADS, head_body, 0)
        for k in range(n_slab):
            ata_scr[k] = jnp.dot(ed0_ref[...], xt_scr[k], preferred_element_type=F32)
        hb_scr[...] = jnp.zeros_like(hb_scr)

    tabs = (n1_scr, e1_scr, r2_scr, e2_scr, n_i)
    _pipe_chunk(2 * j, ata_scr, ha_scr, edb_ref, atb_scr, eutp_ref, hb_scr,
                xt_scr, acc_scr, tabs, n_slab)
    _pipe_chunk(2 * j + 1, atb_scr, hb_scr, edn_ref, ata_scr, euta_ref, ha_scr,
                xt_scr, acc_scr, tabs, n_slab)

    @pl.when(j == pl.num_programs(1) - 1)
    def _epilogue():
        y = jnp.concatenate(
            [(acc_scr[k] + jnp.dot(eutl_ref[...], hb_scr[k], preferred_element_type=F32)).T
             for k in range(n_slab)], axis=0)
        gt2 = mod_ref[5:6, :]
        o_ref[...] = x_ref[...] + gt2 * (_rms(y) * gpost_ref[...])


def _peer(x1, mod3, g_pre, g_post, w_query, sub_keys, expert_down, expert_up, seq, tb, ec):
    t, d = x1.shape
    ne = expert_down.shape[0]
    n_lg = tb // LANES
    wqt = w_query.T.astype(BF16)
    sk = sub_keys.reshape(PEER_HEADS * 2, PEER_N_KEYS, PEER_QDIM // 2).astype(BF16)
    ed = expert_down.astype(BF16)
    eut = expert_up.reshape(ne // ec, ec, d).transpose(0, 2, 1).astype(BF16)
    row = lambda a: a.reshape(1, -1)
    full = lambda shape: pl.BlockSpec(shape, lambda i, j: (0,) * len(shape))
    tiles_per_seq = seq // tb
    n_chunks = ne // ec
    last = n_chunks - 1
    table = lambda dt: pltpu.VMEM((PEER_HEADS, n_lg, PEER_N_KEYS, LANES), dt)
    slabbed = lambda rows, dt: pltpu.VMEM(
        (n_lg // LG_PER_SLAB, rows, LG_PER_SLAB * LANES), dt)
    return pl.pallas_call(
        functools.partial(_peer_kernel, tb=tb, ec=ec),
        out_shape=jax.ShapeDtypeStruct((t, d), F32),
        grid=(t // tb, n_chunks // 2),
        in_specs=[
            pl.BlockSpec((tb, d), lambda i, j: (i, 0)),
            pl.BlockSpec((None, 6, d), lambda i, j: (i // tiles_per_seq, 0, 0)),
            full((1, d)), full((1, d)),
            full((PEER_HEADS * PEER_QDIM, d)),
            full((PEER_HEADS * 2, PEER_N_KEYS, PEER_QDIM // 2)),
            pl.BlockSpec((ec, d), lambda i, j: (0, 0)),
            pl.BlockSpec((ec, d), lambda i, j: (2 * j + 1, 0)),
            pl.BlockSpec((ec, d), lambda i, j: (jnp.minimum(2 * j + 2, last), 0)),
            pl.BlockSpec((None, d, ec), lambda i, j: (jnp.maximum(2 * j - 1, 0), 0, 0)),
            pl.BlockSpec((None, d, ec), lambda i, j: (2 * j, 0, 0)),
            pl.BlockSpec((None, d, ec), lambda i, j: (last, 0, 0)),
        ],
        out_specs=pl.BlockSpec((tb, d), lambda i, j: (i, 0)),
        scratch_shapes=[
            slabbed(d, BF16),
            slabbed(d, F32),
            pltpu.VMEM((2, n_lg, PEER_N_KEYS, LANES), F32),
            table(jnp.uint32), table(jnp.uint32), table(BF16), table(BF16),
            slabbed(ec, F32), slabbed(ec, F32),
            slabbed(ec, BF16), slabbed(ec, BF16),
        ],
        compiler_params=pltpu.CompilerParams(
            dimension_semantics=("arbitrary", "arbitrary"), vmem_limit_bytes=VMEM_LIMIT_BYTES),
        name="peer_dense",
    )(x1, mod3, row(g_pre), row(g_post), wqt, sk, ed, ed, ed, eut, eut, eut)


def _pick_tile(n, target):
    t = min(n, target)
    while n % t:
        t -= LANES
    return t


def kernel(x, c, w_ada, b_ada, g_pre_mix, g_post_mix, g_pre_ffn, g_post_ffn, w_in, w_s, b_s,
           sgu_ln_g, sgu_ln_b, sinks, g_att_out, g_sgu_out, w_out, w_query, sub_keys,
           expert_down, expert_up):
    bsz, seq, d = x.shape
    ts = _pick_tile(seq, 512)
    tb = _pick_tile(seq, 512)
    ec = _pick_tile(expert_down.shape[1], 512)
    for l in range(w_ada.shape[0]):
        mod3 = _ada_modulation(c, w_ada[l], b_ada[l]).reshape(bsz, 6, d)
        x = _mixer(x, mod3, g_pre_mix[l], g_post_mix[l], w_in[l], w_s[l], b_s[l],
                   sgu_ln_g[l].reshape(-1), sgu_ln_b[l].reshape(-1), sinks[l],
                   g_att_out[l], g_sgu_out[l], w_out[l], ts)
        y = _peer(x.reshape(bsz * seq, d), mod3, g_pre_ffn[l], g_post_ffn[l], w_query[l],
                  sub_keys[l], expert_down[l], expert_up[l], seq, tb, ec)
        x = y.reshape(bsz, seq, d)
    return x
```
